```python
import jax, jax.numpy as jnp
from jax import lax
import numpy as np

D_MODEL = 1024
BATCH = 2
SEQ = 8192
DEPTH = 4

CHUNK = 64
N_MIXERS = 2
EPS = 1e-6

SGU_BLOCK = 128
GMLP_HIDDEN = 2 * D_MODEL
SGU_GROUPS = 8
SGU_GROUP_DIM = GMLP_HIDDEN // SGU_GROUPS

N_HEADS = 16
HEAD_DIM = D_MODEL // N_HEADS
LEFT_CHUNKS = 8
BAND = (LEFT_CHUNKS + 1) * CHUNK
REL_MIN = -(CHUNK - 1)
REL_MAX = 128
N_REL = REL_MAX - REL_MIN + 1

D_FF = -(-(8 * D_MODEL) // (3 * 256)) * 256

N_A = (DEPTH + 1) // 2
N_B = DEPTH // 2

kernel_name = "hybrid_gmlp_chunkattn_trunk"


def rms_norm(x, g):
    xf = x.astype(jnp.float32)
    y = xf * lax.rsqrt(jnp.mean(xf * xf, axis=-1, keepdims=True) + EPS)
    return (y * g.astype(jnp.float32)).astype(x.dtype)


def gmlp_mixer(h, w_in, v_gain, w_s, b_s, w_out):
    bsz, seq, _ = h.shape
    uv = jax.nn.gelu(h @ w_in)
    u, v = jnp.split(uv, 2, axis=-1)
    v = rms_norm(v, v_gain)
    nb = seq // SGU_BLOCK
    v = v.reshape(bsz, nb, SGU_BLOCK, SGU_GROUPS, SGU_GROUP_DIM)
    pos = jnp.arange(SGU_BLOCK)
    mask = (pos[None, :] // CHUNK) <= (pos[:, None] // CHUNK)
    w = w_s * mask.astype(w_s.dtype)[None]
    v = jnp.einsum('gpq,bnqgc->bnpgc', w, v) + b_s.T[None, None, :, :, None]
    y = u * v.reshape(bsz, seq, GMLP_HIDDEN)
    return y @ w_out


def chunk_attention(h, w_qkv, rel_bias, w_out):
    bsz, seq, _ = h.shape
    nc = seq // CHUNK
    qkv = (h @ w_qkv).reshape(bsz, seq, 3, N_HEADS, HEAD_DIM)
    q, k, v = qkv[:, :, 0], qkv[:, :, 1], qkv[:, :, 2]
    pad = LEFT_CHUNKS * CHUNK
    kp = jnp.pad(k, ((0, 0), (pad, 0), (0, 0), (0, 0)))
    vp = jnp.pad(v, ((0, 0), (pad, 0), (0, 0), (0, 0)))
    qi = jnp.arange(CHUNK)[:, None]
    kj = jnp.arange(BAND)[None, :]
    rel_idx = jnp.clip(qi - (kj - pad), REL_MIN, REL_MAX) - REL_MIN
    bias = rel_bias[:, rel_idx].astype(jnp.float32)
    scale = HEAD_DIM ** -0.5

    def one_chunk(c):
        start = c * CHUNK
        qc = lax.dynamic_slice_in_dim(q, start, CHUNK, axis=1)
        kc = lax.dynamic_slice_in_dim(kp, start, BAND, axis=1)
        vc = lax.dynamic_slice_in_dim(vp, start, BAND, axis=1)
        s = jnp.einsum('bqhd,bkhd->bhqk', qc, kc).astype(jnp.float32) * scale + bias
        valid = kj >= pad - start
        s = jnp.where(valid[None, None], s, -jnp.inf)
        p = jax.nn.softmax(s, axis=-1).astype(vc.dtype)
        return jnp.einsum('bhqk,bkhd->bqhd', p, vc)

    o = lax.map(one_chunk, jnp.arange(nc))
    o = jnp.moveaxis(o, 0, 1).reshape(bsz, seq, N_HEADS * HEAD_DIM)
    return o @ w_out


def swiglu(h, w_gate, w_up, w_down):
    return (jax.nn.silu(h @ w_gate) * (h @ w_up)) @ w_down


def setup_inputs(seed: int = 0) -> dict:
    key = jax.random.key(seed)
    ks = jax.random.split(key, 16)
    f32 = jnp.float32

    def nrm(k, shape, scale):
        return jax.random.normal(k, shape, f32) * scale

    return {
        "x": jax.random.normal(ks[0], (BATCH, SEQ, D_MODEL), f32),
        "norm_mix_g": 1.0 + nrm(ks[1], (DEPTH, D_MODEL), 0.05),
        "norm_ffn_g": 1.0 + nrm(ks[2], (DEPTH, D_MODEL), 0.05),
        "final_g": 1.0 + nrm(ks[3], (D_MODEL,), 0.05),
        "a_w_in": nrm(ks[4], (N_A, D_MODEL, 2 * GMLP_HIDDEN), D_MODEL ** -0.5),
        "a_v_gain": 1.0 + nrm(ks[5], (N_A, GMLP_HIDDEN), 0.05),
        "a_w_s": nrm(ks[6], (N_A, SGU_GROUPS, SGU_BLOCK, SGU_BLOCK), 0.5 * SGU_BLOCK ** -0.5),
        "a_b_s": 1.0 + nrm(ks[7], (N_A, SGU_GROUPS, SGU_BLOCK), 0.1),
        "a_w_out": nrm(ks[8], (N_A, GMLP_HIDDEN, D_MODEL), GMLP_HIDDEN ** -0.5),
        "b_w_qkv": nrm(ks[9], (N_B, D_MODEL, 3 * N_HEADS * HEAD_DIM), D_MODEL ** -0.5),
        "b_rel_bias": nrm(ks[10], (N_B, N_HEADS, N_REL), 0.5),
        "b_w_out": nrm(ks[11], (N_B, N_HEADS * HEAD_DIM, D_MODEL), (N_HEADS * HEAD_DIM) ** -0.5),
        "ffn_w_gate": nrm(ks[12], (DEPTH, D_MODEL, D_FF), D_MODEL ** -0.5),
        "ffn_w_up": nrm(ks[13], (DEPTH, D_MODEL, D_FF), D_MODEL ** -0.5),
        "ffn_w_down": nrm(ks[14], (DEPTH, D_FF, D_MODEL), D_FF ** -0.5),
    }


def reference(x, norm_mix_g, norm_ffn_g, final_g, a_w_in, a_v_gain, a_w_s, a_b_s,
              a_w_out, b_w_qkv, b_rel_bias, b_w_out, ffn_w_gate, ffn_w_up, ffn_w_down):
    for i in range(DEPTH):
        j = i // N_MIXERS
        hn = rms_norm(x, norm_mix_g[i])
        if i % N_MIXERS == 0:
            x = x + gmlp_mixer(hn, a_w_in[j], a_v_gain[j], a_w_s[j], a_b_s[j], a_w_out[j])
        else:
            x = x + chunk_attention(hn, b_w_qkv[j], b_rel_bias[j], b_w_out[j])
        hn = rms_norm(x, norm_ffn_g[i])
        x = x + swiglu(hn, ffn_w_gate[i], ffn_w_up[i], ffn_w_down[i])
    return rms_norm(x, final_g)
```

```python
import functools

import numpy as np
import jax
import jax.numpy as jnp
from jax import lax
from jax.experimental import pallas as pl
from jax.experimental.pallas import tpu as pltpu

F32 = jnp.float32
BF16 = jnp.bfloat16

EPS = 1e-6
CHUNK = 64
SGU_BLOCK = 128
SGU_GROUPS = 8
N_HEADS = 16
HEAD_DIM = 64
LEFT_CHUNKS = 8
REL_MIN = -(CHUNK - 1)
REL_MAX = 128
N_REL = REL_MAX - REL_MIN + 1

Q_TILE = 2 * CHUNK
KV_PAD = LEFT_CHUNKS * CHUNK
KV_WIN = KV_PAD + Q_TILE
HEADS_PER_GROUP = 4
GROUP_W = HEADS_PER_GROUP * HEAD_DIM
ROLL_W = 1024

TOKEN_TILE = 512
V7X_VMEM_LIMIT_BYTES = 56 * 1024 * 1024


def _rms(x, g):
    ms = jnp.mean(x * x, axis=-1, keepdims=True)
    return x * lax.rsqrt(ms + EPS) * g


def _gelu_tanh(x):
    c = np.float32(np.sqrt(2.0 / np.pi))
    return x * (0.5 * (1.0 + jnp.tanh(c * (x + np.float32(0.044715) * (x * x * x)))))


def _resident(shape):
    nd = len(shape)
    return pl.BlockSpec(shape, lambda *_: (0,) * nd, pipeline_mode=pl.Buffered(1))


def _params(n_axes):
    return pltpu.CompilerParams(
        dimension_semantics=("arbitrary",) * n_axes,
        vmem_limit_bytes=V7X_VMEM_LIMIT_BYTES,
    )


def _ffn_kernel(*refs, has_pre, has_final, chunks):
    it = iter(refs)
    x_ref = next(it)
    if has_pre:
        o_ref, wo_ref = next(it), next(it)
    g_ref, wg_ref, wu_ref, wd_ref = next(it), next(it), next(it), next(it)
    if has_final:
        fg_ref = next(it)
    out_ref = next(it)

    x = x_ref[...]
    if has_pre:
        x = x + jnp.dot(o_ref[...], wo_ref[...], preferred_element_type=F32)
    hn = _rms(x, g_ref[...]).astype(BF16)
    acc = x
    for c0, cw in chunks:
        gate = jnp.dot(hn, wg_ref[:, c0:c0 + cw], preferred_element_type=F32)
        up = jnp.dot(hn, wu_ref[:, c0:c0 + cw], preferred_element_type=F32)
        h = (gate * (0.5 * (1.0 + jnp.tanh(0.5 * gate))) * up).astype(BF16)
        acc = acc + jnp.dot(h, wd_ref[c0:c0 + cw, :], preferred_element_type=F32)
    if has_final:
        acc = _rms(acc, fg_ref[...])
    out_ref[...] = acc


def _ffn(x2d, g, wg, wu, wd, pre=None, final_g=None):
    m, d = x2d.shape
    f = wg.shape[1]
    chunk_w = 512
    chunks = tuple((c0, min(chunk_w, f - c0)) for c0 in range(0, f, chunk_w))
    row_spec = pl.BlockSpec((TOKEN_TILE, d), lambda i: (i, 0))
    args, specs = [x2d], [row_spec]
    if pre is not None:
        o2d, wo = pre
        args += [o2d, wo]
        specs += [row_spec, _resident(wo.shape)]
    args += [g.reshape(1, d), wg, wu, wd]
    specs += [_resident((1, d)), _resident(wg.shape), _resident(wu.shape), _resident(wd.shape)]
    if final_g is not None:
        args.append(final_g.reshape(1, d))
        specs.append(_resident((1, d)))
    return pl.pallas_call(
        functools.partial(_ffn_kernel, has_pre=pre is not None,
                          has_final=final_g is not None, chunks=chunks),
        grid=(m // TOKEN_TILE,),
        in_specs=specs,
        out_specs=row_spec,
        out_shape=jax.ShapeDtypeStruct((m, d), F32),
        compiler_params=_params(1),
        name="ffn",
    )(*args)


def _gmlp_kernel(x_ref, g_ref, win_ref, vg_ref, ws_ref, bs_ref, wout_ref, out_ref,
                 u_scr, v_scr, y_scr):
    tm = x_ref.shape[0]
    hid = u_scr.shape[1]
    cw = 512
    x = x_ref[...]
    hn = _rms(x, g_ref[...]).astype(BF16)
    ssq = jnp.zeros((tm, 1), F32)
    for c in range(2 * hid // cw):
        t = _gelu_tanh(jnp.dot(hn, win_ref[:, c * cw:(c + 1) * cw], preferred_element_type=F32))
        if c * cw < hid:
            u_scr[:, c * cw:(c + 1) * cw] = t
        else:
            v_scr[:, c * cw - hid:(c + 1) * cw - hid] = t
            ssq = ssq + jnp.sum(t * t, axis=-1, keepdims=True)
    inv = lax.rsqrt(ssq * (1.0 / hid) + EPS)

    p_chunk = lax.broadcasted_iota(jnp.int32, (SGU_BLOCK, SGU_BLOCK), 0) // CHUNK
    q_chunk = lax.broadcasted_iota(jnp.int32, (SGU_BLOCK, SGU_BLOCK), 1) // CHUNK
    causal = q_chunk <= p_chunk
    gw = hid // SGU_GROUPS
    for grp in range(SGU_GROUPS):
        cols = slice(grp * gw, (grp + 1) * gw)
        w_g = jnp.where(causal, ws_ref[grp], 0.0).astype(BF16)
        bias = bs_ref[:, grp:grp + 1]
        for n in range(tm // SGU_BLOCK):
            rows = slice(n * SGU_BLOCK, (n + 1) * SGU_BLOCK)
            vn = (v_scr[rows, cols] * inv[rows] * vg_ref[:, cols]).astype(BF16)
            mixed = jnp.dot(w_g, vn, preferred_element_type=F32) + bias
            y_scr[rows, cols] = (u_scr[rows, cols] * mixed).astype(BF16)
    out_ref[...] = x + jnp.dot(y_scr[...], wout_ref[...], preferred_element_type=F32)


def _gmlp(x2d, g, w_in, v_gain, w_s, b_s, w_out):
    m, d = x2d.shape
    hid = w_out.shape[0]
    row_spec = pl.BlockSpec((TOKEN_TILE, d), lambda i: (i, 0))
    return pl.pallas_call(
        _gmlp_kernel,
        grid=(m // TOKEN_TILE,),
        in_specs=[row_spec, _resident((1, d)), _resident(w_in.shape), _resident((1, hid)),
                  _resident(w_s.shape), _resident((SGU_BLOCK, SGU_GROUPS)), _resident(w_out.shape)],
        out_specs=row_spec,
        out_shape=jax.ShapeDtypeStruct((m, d), F32),
        scratch_shapes=[pltpu.VMEM((TOKEN_TILE, hid), F32), pltpu.VMEM((TOKEN_TILE, hid), F32),
                        pltpu.VMEM((TOKEN_TILE, hid), BF16)],
        compiler_params=_params(1),
        name="gmlp",
    )(x2d, g.reshape(1, d), w_in, v_gain.reshape(1, hid), w_s, b_s.T, w_out)


def _qkv_kernel(x_ref, g_ref, w_ref, out_ref):
    d = x_ref.shape[-1]

    @pl.when(pl.program_id(1) == 0)
    def _():
        out_ref[...] = jnp.zeros(out_ref.shape, out_ref.dtype)

    @pl.when(pl.program_id(1) > 0)
    def _():
        hn = _rms(x_ref[0], g_ref[...]).astype(BF16)
        qkv = jnp.dot(hn, w_ref[...], preferred_element_type=F32)
        out_ref[0, :, :d] = (qkv[:, :d] * (HEAD_DIM ** -0.5)).astype(BF16)
        out_ref[0, :, d:] = qkv[:, d:].astype(BF16)


def _qkv(x3d, g, w_qkv):
    b, s, d = x3d.shape
    assert KV_PAD == TOKEN_TILE
    return pl.pallas_call(
        _qkv_kernel,
        grid=(b, s // TOKEN_TILE + 1),
        in_specs=[pl.BlockSpec((1, TOKEN_TILE, d), lambda bi, r: (bi, jnp.maximum(r - 1, 0), 0)),
                  _resident((1, d)), _resident(w_qkv.shape)],
        out_specs=pl.BlockSpec((1, TOKEN_TILE, 3 * d), lambda bi, r: (bi, r, 0)),
        out_shape=jax.ShapeDtypeStruct((b, s + KV_PAD, 3 * d), BF16),
        compiler_params=_params(2),
        name="qkv_proj",
    )(x3d, g.reshape(1, d), w_qkv)


def _bias_kernel(rb_ref, out_ref):
    h = pl.program_id(0)
    m = lax.broadcasted_iota(jnp.int32, (8, ROLL_W), 1)
    idx = jnp.where(m < ROLL_W - Q_TILE, jnp.clip(KV_PAD - m, REL_MIN, REL_MAX) - REL_MIN, N_REL - 1)
    row = jnp.zeros((8, ROLL_W), F32)
    for k in range(N_REL):
        row = jnp.where(idx == k, rb_ref[h, k], row)
    tile = jnp.concatenate([row] * (Q_TILE // 8), axis=0)
    i = lax.broadcasted_iota(jnp.int32, (Q_TILE, ROLL_W), 0)
    for bit in range(Q_TILE.bit_length() - 1):
        tile = jnp.where(((i >> bit) & 1) == 1, pltpu.roll(tile, 1 << bit, 1), tile)
    tile = tile[:, :KV_WIN]
    i = lax.broadcasted_iota(jnp.int32, (Q_TILE, KV_WIN), 0)
    j = lax.broadcasted_iota(jnp.int32, (Q_TILE, KV_WIN), 1)
    first = (i // CHUNK) * CHUNK
    in_band = (j >= first) & (j < first + KV_PAD + CHUNK)
    out_ref[0] = jnp.where(in_band, tile, -jnp.inf)


def _bias_tiles(rel_bias):
    nh = rel_bias.shape[0]
    return pl.pallas_call(
        _bias_kernel,
        grid=(nh,),
        in_specs=[pl.BlockSpec(memory_space=pltpu.SMEM)],
        out_specs=pl.BlockSpec((1, Q_TILE, KV_WIN), lambda h: (h, 0, 0)),
        out_shape=jax.ShapeDtypeStruct((nh, Q_TILE, KV_WIN), F32),
        compiler_params=_params(1),
        name="rel_bias_tiles",
    )(rel_bias)


def _attn_kernel(q_ref, k_ref, v_ref, bias_ref, o_ref):
    t = pl.program_id(2)
    start = pl.multiple_of(t * Q_TILE, Q_TILE)
    q = q_ref[0]
    kw = k_ref[0, pl.ds(start, KV_WIN), :]
    vw = v_ref[0, pl.ds(start, KV_WIN), :]
    lane_head = lax.broadcasted_iota(jnp.int32, (1, GROUP_W), 1) // HEAD_DIM
    col = lax.broadcasted_iota(jnp.int32, (Q_TILE, KV_WIN), 1)
    valid = col >= KV_PAD - start
    acc = jnp.zeros((Q_TILE, GROUP_W), F32)
    for h in range(HEADS_PER_GROUP):
        mine = lane_head == h
        qh = jnp.where(mine, q, jnp.zeros_like(q))
        s = lax.dot_general(qh, kw, (((1,), (1,)), ((), ())), preferred_element_type=F32)
        s = jnp.where(valid, s + bias_ref[h], -jnp.inf)
        p = jnp.exp(s - jnp.max(s, axis=-1, keepdims=True))
        denom = jnp.sum(p, axis=-1, keepdims=True)
        vh = jnp.where(mine, vw, jnp.zeros_like(vw))
        acc = acc + jnp.dot(p.astype(BF16), vh, preferred_element_type=F32) / denom
    o_ref[0] = acc.astype(BF16)


def _attention(qkv, bias_tiles, seq):
    b, s_pad, d3 = qkv.shape
    d = d3 // 3
    n_groups = d // GROUP_W
    kv_spec = lambda off: pl.BlockSpec((1, s_pad, GROUP_W), lambda bi, g, t: (bi, 0, off + g))
    return pl.pallas_call(
        _attn_kernel,
        grid=(b, n_groups, seq // Q_TILE),
        in_specs=[pl.BlockSpec((1, Q_TILE, GROUP_W), lambda bi, g, t: (bi, t + KV_PAD // Q_TILE, g)),
                  kv_spec(n_groups), kv_spec(2 * n_groups),
                  pl.BlockSpec((HEADS_PER_GROUP, Q_TILE, KV_WIN), lambda bi, g, t: (g, 0, 0))],
        out_specs=pl.BlockSpec((1, Q_TILE, GROUP_W), lambda bi, g, t: (bi, t, g)),
        out_shape=jax.ShapeDtypeStruct((b, seq, d), BF16),
        compiler_params=_params(3),
        name="band_attention",
    )(qkv, qkv, qkv, bias_tiles)


def kernel(x, norm_mix_g, norm_ffn_g, final_g, a_w_in, a_v_gain, a_w_s, a_b_s, a_w_out,
           b_w_qkv, b_rel_bias, b_w_out, ffn_w_gate, ffn_w_up, ffn_w_down):
    bsz, seq, d = x.shape
    depth = norm_mix_g.shape[0]
    n_mixers = 2
    assert d == N_HEADS * HEAD_DIM and seq % TOKEN_TILE == 0

    bf = lambda w: w.astype(BF16)
    a_w_in, a_w_out, b_w_qkv, b_w_out = bf(a_w_in), bf(a_w_out), bf(b_w_qkv), bf(b_w_out)
    ffn_w_gate, ffn_w_up, ffn_w_down = bf(ffn_w_gate), bf(ffn_w_up), bf(ffn_w_down)

    x2d = x.reshape(bsz * seq, d)
    for i in range(depth):
        j = i // n_mixers
        final = final_g if i == depth - 1 else None
        if i % n_mixers == 0:
            x2d = _gmlp(x2d, norm_mix_g[i], a_w_in[j], a_v_gain[j], a_w_s[j], a_b_s[j], a_w_out[j])
            x2d = _ffn(x2d, norm_ffn_g[i], ffn_w_gate[i], ffn_w_up[i], ffn_w_down[i], final_g=final)
        else:
            qkv = _qkv(x2d.reshape(bsz, seq, d), norm_mix_g[i], b_w_qkv[j])
            o = _attention(qkv, _bias_tiles(b_rel_bias[j]), seq)
            x2d = _ffn(x2d, norm_ffn_g[i], ffn_w_gate[i], ffn_w_up[i], ffn_w_down[i],
                       pre=(o.reshape(bsz * seq, d), b_w_out[j]), final_g=final)
    return x2d.reshape(bsz, seq, d)
```

```python
import functools

import numpy as np
import jax
import jax.numpy as jnp
from jax import lax
from jax.experimental import pallas as pl
from jax.experimental.pallas import tpu as pltpu

F32 = jnp.float32
BF16 = jnp.bfloat16

EPS = 1e-6
CHUNK = 64
SGU_BLOCK = 128
SGU_GROUPS = 8
N_HEADS = 16
HEAD_DIM = 64
LEFT_CHUNKS = 8
REL_MIN = -(CHUNK - 1)
REL_MAX = 128
N_REL = REL_MAX - REL_MIN + 1

Q_TILE = 2 * CHUNK
KV_PAD = LEFT_CHUNKS * CHUNK
KV_WIN = KV_PAD + Q_TILE
HEADS_PER_GROUP = 4
GROUP_W = HEADS_PER_GROUP * HEAD_DIM
ROLL_W = 1024

TOKEN_TILE = 512
V7X_VMEM_LIMIT_BYTES = 56 * 1024 * 1024


def _rms(x, g):
    ms = jnp.mean(x * x, axis=-1, keepdims=True)
    return x * lax.rsqrt(ms + EPS) * g


def _gelu_tanh(x):
    c = np.float32(np.sqrt(2.0 / np.pi))
    return x * (0.5 * (1.0 + jnp.tanh(c * (x + np.float32(0.044715) * (x * x * x)))))


def _resident(shape):
    nd = len(shape)
    return pl.BlockSpec(shape, lambda *_: (0,) * nd, pipeline_mode=pl.Buffered(1))


def _params(n_axes):
    return pltpu.CompilerParams(
        dimension_semantics=("arbitrary",) * n_axes,
        vmem_limit_bytes=V7X_VMEM_LIMIT_BYTES,
    )


def _ffn_kernel(*refs, has_pre, has_final, chunks):
    it = iter(refs)
    x_ref = next(it)
    if has_pre:
        o_ref, wo_ref = next(it), next(it)
    g_ref, wg_ref, wu_ref, wd_ref = next(it), next(it), next(it), next(it)
    if has_final:
        fg_ref = next(it)
    out_ref = next(it)

    x = x_ref[...]
    if has_pre:
        x = x + jnp.dot(o_ref[...], wo_ref[...], preferred_element_type=F32)
    hn = _rms(x, g_ref[...]).astype(BF16)
    acc = x
    for c0, cw in chunks:
        gate = jnp.dot(hn, wg_ref[:, c0:c0 + cw], preferred_element_type=F32)
        up = jnp.dot(hn, wu_ref[:, c0:c0 + cw], preferred_element_type=F32)
        h = (gate * (0.5 * (1.0 + jnp.tanh(0.5 * gate))) * up).astype(BF16)
        acc = acc + jnp.dot(h, wd_ref[c0:c0 + cw, :], preferred_element_type=F32)
    if has_final:
        acc = _rms(acc, fg_ref[...])
    out_ref[...] = acc


def _ffn(x2d, g, wg, wu, wd, pre=None, final_g=None):
    m, d = x2d.shape
    f = wg.shape[1]
    chunk_w = 512
    chunks = tuple((c0, min(chunk_w, f - c0)) for c0 in range(0, f, chunk_w))
    row_spec = pl.BlockSpec((TOKEN_TILE, d), lambda i: (i, 0))
    args, specs = [x2d], [row_spec]
    if pre is not None:
        o2d, wo = pre
        args += [o2d, wo]
        specs += [row_spec, _resident(wo.shape)]
    args += [g.reshape(1, d), wg, wu, wd]
    specs += [_resident((1, d)), _resident(wg.shape), _resident(wu.shape), _resident(wd.shape)]
    if final_g is not None:
        args.append(final_g.reshape(1, d))
        specs.append(_resident((1, d)))
    return pl.pallas_call(
        functools.partial(_ffn_kernel, has_pre=pre is not None,
                          has_final=final_g is not None, chunks=chunks),
        grid=(m // TOKEN_TILE,),
        in_specs=specs,
        out_specs=row_spec,
        out_shape=jax.ShapeDtypeStruct((m, d), F32),
        compiler_params=_params(1),
        name="ffn",
    )(*args)


def _gmlp_kernel(x_ref, g_ref, win_ref, vg_ref, ws_ref, bs_ref, wout_ref, out_ref,
                 u_scr, v_scr, y_scr):
    tm = x_ref.shape[0]
    hid = u_scr.shape[1]
    cw = 512
    x = x_ref[...]
    hn = _rms(x, g_ref[...]).astype(BF16)
    ssq = jnp.zeros((tm, 1), F32)
    for c in range(2 * hid // cw):
        t = _gelu_tanh(jnp.dot(hn, win_ref[:, c * cw:(c + 1) * cw], preferred_element_type=F32))
        if c * cw < hid:
            u_scr[:, c * cw:(c + 1) * cw] = t
        else:
            v_scr[:, c * cw - hid:(c + 1) * cw - hid] = t
            ssq = ssq + jnp.sum(t * t, axis=-1, keepdims=True)
    inv = lax.rsqrt(ssq * (1.0 / hid) + EPS)

    p_chunk = lax.broadcasted_iota(jnp.int32, (SGU_BLOCK, SGU_BLOCK), 0) // CHUNK
    q_chunk = lax.broadcasted_iota(jnp.int32, (SGU_BLOCK, SGU_BLOCK), 1) // CHUNK
    causal = q_chunk <= p_chunk
    gw = hid // SGU_GROUPS
    for grp in range(SGU_GROUPS):
        cols = slice(grp * gw, (grp + 1) * gw)
        w_g = jnp.where(causal, ws_ref[grp], 0.0).astype(BF16)
        bias = bs_ref[:, grp:grp + 1]
        for n in range(tm // SGU_BLOCK):
            rows = slice(n * SGU_BLOCK, (n + 1) * SGU_BLOCK)
            vn = (v_scr[rows, cols] * inv[rows] * vg_ref[:, cols]).astype(BF16)
            mixed = jnp.dot(w_g, vn, preferred_element_type=F32) + bias
            y_scr[rows, cols] = (u_scr[rows, cols] * mixed).astype(BF16)
    out_ref[...] = x + jnp.dot(y_scr[...], wout_ref[...], preferred_element_type=F32)


def _gmlp(x2d, g, w_in, v_gain, w_s, b_s, w_out):
    m, d = x2d.shape
    hid = w_out.shape[0]
    row_spec = pl.BlockSpec((TOKEN_TILE, d), lambda i: (i, 0))
    return pl.pallas_call(
        _gmlp_kernel,
        grid=(m // TOKEN_TILE,),
        in_specs=[row_spec, _resident((1, d)), _resident(w_in.shape), _resident((1, hid)),
                  _resident(w_s.shape), _resident((SGU_BLOCK, SGU_GROUPS)), _resident(w_out.shape)],
        out_specs=row_spec,
        out_shape=jax.ShapeDtypeStruct((m, d), F32),
        scratch_shapes=[pltpu.VMEM((TOKEN_TILE, hid), F32), pltpu.VMEM((TOKEN_TILE, hid), F32),
                        pltpu.VMEM((TOKEN_TILE, hid), BF16)],
        compiler_params=_params(1),
        name="gmlp",
    )(x2d, g.reshape(1, d), w_in, v_gain.reshape(1, hid), w_s, b_s.T, w_out)


def _qkv_kernel(x_ref, g_ref, w_ref, out_ref):
    d = x_ref.shape[-1]

    @pl.when(pl.program_id(1) == 0)
    def _():
        out_ref[...] = jnp.zeros(out_ref.shape, out_ref.dtype)

    @pl.when(pl.program_id(1) > 0)
    def _():
        hn = _rms(x_ref[0], g_ref[...]).astype(BF16)
        qkv = jnp.dot(hn, w_ref[...], preferred_element_type=F32)
        out_ref[0, :, :d] = (qkv[:, :d] * (HEAD_DIM ** -0.5)).astype(BF16)
        out_ref[0, :, d:] = qkv[:, d:].astype(BF16)


def _qkv(x3d, g, w_qkv):
    b, s, d = x3d.shape
    assert KV_PAD == TOKEN_TILE
    return pl.pallas_call(
        _qkv_kernel,
        grid=(b, s // TOKEN_TILE + 1),
        in_specs=[pl.BlockSpec((1, TOKEN_TILE, d), lambda bi, r: (bi, jnp.maximum(r - 1, 0), 0)),
                  _resident((1, d)), _resident(w_qkv.shape)],
        out_specs=pl.BlockSpec((1, TOKEN_TILE, 3 * d), lambda bi, r: (bi, r, 0)),
        out_shape=jax.ShapeDtypeStruct((b, s + KV_PAD, 3 * d), BF16),
        compiler_params=_params(2),
        name="qkv_proj",
    )(x3d, g.reshape(1, d), w_qkv)


def _bias_kernel(rb_ref, out_ref):
    h = pl.program_id(0)
    m = lax.broadcasted_iota(jnp.int32, (8, ROLL_W), 1)
    idx = jnp.where(m < ROLL_W - Q_TILE, jnp.clip(KV_PAD - m, REL_MIN, REL_MAX) - REL_MIN, N_REL - 1)
    row = jnp.zeros((8, ROLL_W), F32)
    for k in range(N_REL):
        row = jnp.where(idx == k, rb_ref[h, k], row)
    tile = jnp.concatenate([row] * (Q_TILE // 8), axis=0)
    i = lax.broadcasted_iota(jnp.int32, (Q_TILE, ROLL_W), 0)
    for bit in range(Q_TILE.bit_length() - 1):
        tile = jnp.where(((i >> bit) & 1) == 1, pltpu.roll(tile, 1 << bit, 1), tile)
    tile = tile[:, :KV_WIN]
    i = lax.broadcasted_iota(jnp.int32, (Q_TILE, KV_WIN), 0)
    j = lax.broadcasted_iota(jnp.int32, (Q_TILE, KV_WIN), 1)
    first = (i // CHUNK) * CHUNK
    in_band = (j >= first) & (j < first + KV_PAD + CHUNK)
    out_ref[0] = jnp.where(in_band, tile, -jnp.inf)


def _bias_tiles(rel_bias):
    nh = rel_bias.shape[0]
    return pl.pallas_call(
        _bias_kernel,
        grid=(nh,),
        in_specs=[pl.BlockSpec(memory_space=pltpu.SMEM)],
        out_specs=pl.BlockSpec((1, Q_TILE, KV_WIN), lambda h: (h, 0, 0)),
        out_shape=jax.ShapeDtypeStruct((nh, Q_TILE, KV_WIN), F32),
        compiler_params=_params(1),
        name="rel_bias_tiles",
    )(rel_bias)


def _attn_kernel(q_ref, k_ref, v_ref, bias_ref, o_ref):
    t = pl.program_id(1)
    start = pl.multiple_of(t * Q_TILE, Q_TILE)
    lane_head = lax.broadcasted_iota(jnp.int32, (1, GROUP_W), 1) // HEAD_DIM
    head_lanes = [lane_head == h for h in range(HEADS_PER_GROUP)]
    col = lax.broadcasted_iota(jnp.int32, (1, KV_WIN), 1)
    valid = col >= KV_PAD - start
    rows = HEADS_PER_GROUP * Q_TILE
    for g in range(q_ref.shape[-1] // GROUP_W):
        cols = slice(g * GROUP_W, (g + 1) * GROUP_W)
        q = q_ref[0, :, cols]
        kw = k_ref[0, pl.ds(start, KV_WIN), cols]
        vw = v_ref[0, pl.ds(start, KV_WIN), cols]
        qs = jnp.concatenate([jnp.where(m, q, jnp.zeros_like(q)) for m in head_lanes], axis=0)
        s = lax.dot_general(qs, kw, (((1,), (1,)), ((), ())), preferred_element_type=F32)
        s = s + bias_ref[g * HEADS_PER_GROUP:(g + 1) * HEADS_PER_GROUP].reshape(rows, KV_WIN)
        s = jnp.where(valid, s, -jnp.inf)
        p = jnp.exp(s - jnp.max(s, axis=-1, keepdims=True))
        inv = 1.0 / jnp.sum(p, axis=-1, keepdims=True)
        r = jnp.dot(p.astype(BF16), vw, preferred_element_type=F32) * inv
        o = jnp.zeros((Q_TILE, GROUP_W), F32)
        for h, m in enumerate(head_lanes):
            o = jnp.where(m, r[h * Q_TILE:(h + 1) * Q_TILE], o)
        o_ref[0, :, cols] = o.astype(BF16)


def _attention(qkv, bias_tiles, seq):
    b, s_pad, d3 = qkv.shape
    d = d3 // 3
    kv_spec = lambda which: pl.BlockSpec((1, s_pad, d), lambda bi, t: (bi, 0, which),
                                         pipeline_mode=pl.Buffered(1))
    return pl.pallas_call(
        _attn_kernel,
        grid=(b, seq // Q_TILE),
        in_specs=[pl.BlockSpec((1, Q_TILE, d), lambda bi, t: (bi, t + KV_PAD // Q_TILE, 0)),
                  kv_spec(1), kv_spec(2), _resident(bias_tiles.shape)],
        out_specs=pl.BlockSpec((1, Q_TILE, d), lambda bi, t: (bi, t, 0)),
        out_shape=jax.ShapeDtypeStruct((b, seq, d), BF16),
        compiler_params=_params(2),
        name="band_attention",
    )(qkv, qkv, qkv, bias_tiles)


def kernel(x, norm_mix_g, norm_ffn_g, final_g, a_w_in, a_v_gain, a_w_s, a_b_s, a_w_out,
           b_w_qkv, b_rel_bias, b_w_out, ffn_w_gate, ffn_w_up, ffn_w_down):
    bsz, seq, d = x.shape
    depth = norm_mix_g.shape[0]
    n_mixers = 2
    assert d == N_HEADS * HEAD_DIM and seq % TOKEN_TILE == 0

    bf = lambda w: w.astype(BF16)
    a_w_in, a_w_out, b_w_qkv, b_w_out = bf(a_w_in), bf(a_w_out), bf(b_w_qkv), bf(b_w_out)
    ffn_w_gate, ffn_w_up, ffn_w_down = bf(ffn_w_gate), bf(ffn_w_up), bf(ffn_w_down)

    x2d = x.reshape(bsz * seq, d)
    for i in range(depth):
        j = i // n_mixers
        final = final_g if i == depth - 1 else None
        if i % n_mixers == 0:
            x2d = _gmlp(x2d, norm_mix_g[i], a_w_in[j], a_v_gain[j], a_w_s[j], a_b_s[j], a_w_out[j])
            x2d = _ffn(x2d, norm_ffn_g[i], ffn_w_gate[i], ffn_w_up[i], ffn_w_down[i], final_g=final)
        else:
            qkv = _qkv(x2d.reshape(bsz, seq, d), norm_mix_g[i], b_w_qkv[j])
            o = _attention(qkv, _bias_tiles(b_rel_bias[j]), seq)
            x2d = _ffn(x2d, norm_ffn_g[i], ffn_w_gate[i], ffn_w_up[i], ffn_w_down[i],
                       pre=(o.reshape(bsz * seq, d), b_w_out[j]), final_g=final)
    return x2d.reshape(bsz, seq, d)
```

```python
import functools

import numpy as np
import jax
import jax.numpy as jnp
from jax import lax
from jax.experimental import pallas as pl
from jax.experimental.pallas import tpu as pltpu

F32 = jnp.float32
BF16 = jnp.bfloat16

EPS = 1e-6
CHUNK = 64
SGU_BLOCK = 128
SGU_GROUPS = 8
N_HEADS = 16
HEAD_DIM = 64
LEFT_CHUNKS = 8
REL_MIN = -(CHUNK - 1)
REL_MAX = 128
N_REL = REL_MAX - REL_MIN + 1

Q_TILE = 2 * CHUNK
KV_PAD = LEFT_CHUNKS * CHUNK
KV_WIN = KV_PAD + Q_TILE
HEADS_PER_GROUP = 4
GROUP_W = HEADS_PER_GROUP * HEAD_DIM
TILES_PER_STEP = 2
ROLL_W = 1024

TOKEN_TILE = 512
V7X_VMEM_LIMIT_BYTES = 56 * 1024 * 1024


def _rms(x, g):
    ms = jnp.mean(x * x, axis=-1, keepdims=True)
    return x * lax.rsqrt(ms + EPS) * g


def _gelu_tanh(x):
    c = np.float32(np.sqrt(2.0 / np.pi))
    return x * (0.5 * (1.0 + jnp.tanh(c * (x + np.float32(0.044715) * (x * x * x)))))


def _resident(shape):
    nd = len(shape)
    return pl.BlockSpec(shape, lambda *_: (0,) * nd, pipeline_mode=pl.Buffered(1))


def _params(n_axes):
    return pltpu.CompilerParams(
        dimension_semantics=("arbitrary",) * n_axes,
        vmem_limit_bytes=V7X_VMEM_LIMIT_BYTES,
    )


def _ffn_kernel(*refs, has_pre, has_final, chunks):
    it = iter(refs)
    x_ref = next(it)
    if has_pre:
        o_ref, wo_ref = next(it), next(it)
    g_ref, wg_ref, wu_ref, wd_ref = next(it), next(it), next(it), next(it)
    if has_final:
        fg_ref = next(it)
    out_ref = next(it)

    x = x_ref[...]
    if has_pre:
        x = x + jnp.dot(o_ref[...], wo_ref[...], preferred_element_type=F32)
    hn = _rms(x, g_ref[...]).astype(BF16)
    acc = x
    for c0, cw in chunks:
        gate = jnp.dot(hn, wg_ref[:, c0:c0 + cw], preferred_element_type=F32)
        up = jnp.dot(hn, wu_ref[:, c0:c0 + cw], preferred_element_type=F32)
        h = (gate * (0.5 * (1.0 + jnp.tanh(0.5 * gate))) * up).astype(BF16)
        acc = acc + jnp.dot(h, wd_ref[c0:c0 + cw, :], preferred_element_type=F32)
    if has_final:
        acc = _rms(acc, fg_ref[...])
    out_ref[...] = acc


def _ffn(x2d, g, wg, wu, wd, pre=None, final_g=None):
    m, d = x2d.shape
    f = wg.shape[1]
    chunk_w = 512
    chunks = tuple((c0, min(chunk_w, f - c0)) for c0 in range(0, f, chunk_w))
    row_spec = pl.BlockSpec((TOKEN_TILE, d), lambda i: (i, 0))
    args, specs = [x2d], [row_spec]
    if pre is not None:
        o2d, wo = pre
        args += [o2d, wo]
        specs += [row_spec, _resident(wo.shape)]
    args += [g.reshape(1, d), wg, wu, wd]
    specs += [_resident((1, d)), _resident(wg.shape), _resident(wu.shape), _resident(wd.shape)]
    if final_g is not None:
        args.append(final_g.reshape(1, d))
        specs.append(_resident((1, d)))
    return pl.pallas_call(
        functools.partial(_ffn_kernel, has_pre=pre is not None,
                          has_final=final_g is not None, chunks=chunks),
        grid=(m // TOKEN_TILE,),
        in_specs=specs,
        out_specs=row_spec,
        out_shape=jax.ShapeDtypeStruct((m, d), F32),
        compiler_params=_params(1),
        name="ffn",
    )(*args)


def _gmlp_kernel(x_ref, g_ref, win_ref, vg_ref, ws_ref, bs_ref, wout_ref, out_ref,
                 u_scr, v_scr, y_scr):
    tm = x_ref.shape[0]
    hid = u_scr.shape[1]
    cw = 512
    x = x_ref[...]
    hn = _rms(x, g_ref[...]).astype(BF16)
    ssq = jnp.zeros((tm, 1), F32)
    for c in range(2 * hid // cw):
        t = _gelu_tanh(jnp.dot(hn, win_ref[:, c * cw:(c + 1) * cw], preferred_element_type=F32))
        if c * cw < hid:
            u_scr[:, c * cw:(c + 1) * cw] = t
        else:
            v_scr[:, c * cw - hid:(c + 1) * cw - hid] = t
            ssq = ssq + jnp.sum(t * t, axis=-1, keepdims=True)
    inv = lax.rsqrt(ssq * (1.0 / hid) + EPS)

    p_chunk = lax.broadcasted_iota(jnp.int32, (SGU_BLOCK, SGU_BLOCK), 0) // CHUNK
    q_chunk = lax.broadcasted_iota(jnp.int32, (SGU_BLOCK, SGU_BLOCK), 1) // CHUNK
    causal = q_chunk <= p_chunk
    gw = hid // SGU_GROUPS
    for grp in range(SGU_GROUPS):
        cols = slice(grp * gw, (grp + 1) * gw)
        w_g = jnp.where(causal, ws_ref[grp], 0.0).astype(BF16)
        bias = bs_ref[:, grp:grp + 1]
        for n in range(tm // SGU_BLOCK):
            rows = slice(n * SGU_BLOCK, (n + 1) * SGU_BLOCK)
            vn = (v_scr[rows, cols] * inv[rows] * vg_ref[:, cols]).astype(BF16)
            mixed = jnp.dot(w_g, vn, preferred_element_type=F32) + bias
            y_scr[rows, cols] = (u_scr[rows, cols] * mixed).astype(BF16)
    out_ref[...] = x + jnp.dot(y_scr[...], wout_ref[...], preferred_element_type=F32)


def _gmlp(x2d, g, w_in, v_gain, w_s, b_s, w_out):
    m, d = x2d.shape
    hid = w_out.shape[0]
    row_spec = pl.BlockSpec((TOKEN_TILE, d), lambda i: (i, 0))
    return pl.pallas_call(
        _gmlp_kernel,
        grid=(m // TOKEN_TILE,),
        in_specs=[row_spec, _resident((1, d)), _resident(w_in.shape), _resident((1, hid)),
                  _resident(w_s.shape), _resident((SGU_BLOCK, SGU_GROUPS)), _resident(w_out.shape)],
        out_specs=row_spec,
        out_shape=jax.ShapeDtypeStruct((m, d), F32),
        scratch_shapes=[pltpu.VMEM((TOKEN_TILE, hid), F32), pltpu.VMEM((TOKEN_TILE, hid), F32),
                        pltpu.VMEM((TOKEN_TILE, hid), BF16)],
        compiler_params=_params(1),
        name="gmlp",
    )(x2d, g.reshape(1, d), w_in, v_gain.reshape(1, hid), w_s, b_s.T, w_out)


def _qkv_kernel(x_ref, g_ref, w_ref, out_ref):
    d = x_ref.shape[-1]

    @pl.when(pl.program_id(1) == 0)
    def _():
        out_ref[...] = jnp.zeros(out_ref.shape, out_ref.dtype)

    @pl.when(pl.program_id(1) > 0)
    def _():
        hn = _rms(x_ref[0], g_ref[...]).astype(BF16)
        qkv = jnp.dot(hn, w_ref[...], preferred_element_type=F32)
        out_ref[0, :, :d] = (qkv[:, :d] * (HEAD_DIM ** -0.5)).astype(BF16)
        out_ref[0, :, d:] = qkv[:, d:].astype(BF16)


def _qkv(x3d, g, w_qkv):
    b, s, d = x3d.shape
    assert KV_PAD == TOKEN_TILE
    return pl.pallas_call(
        _qkv_kernel,
        grid=(b, s // TOKEN_TILE + 1),
        in_specs=[pl.BlockSpec((1, TOKEN_TILE, d), lambda bi, r: (bi, jnp.maximum(r - 1, 0), 0)),
                  _resident((1, d)), _resident(w_qkv.shape)],
        out_specs=pl.BlockSpec((1, TOKEN_TILE, 3 * d), lambda bi, r: (bi, r, 0)),
        out_shape=jax.ShapeDtypeStruct((b, s + KV_PAD, 3 * d), BF16),
        compiler_params=_params(2),
        name="qkv_proj",
    )(x3d, g.reshape(1, d), w_qkv)


def _bias_kernel(rb_ref, out_ref):
    h = pl.program_id(0)
    m = lax.broadcasted_iota(jnp.int32, (8, ROLL_W), 1)
    idx = jnp.where(m < ROLL_W - Q_TILE, jnp.clip(KV_PAD - m, REL_MIN, REL_MAX) - REL_MIN, N_REL - 1)
    row = jnp.zeros((8, ROLL_W), F32)
    for k in range(N_REL):
        row = jnp.where(idx == k, rb_ref[h, k], row)
    tile = jnp.concatenate([row] * (Q_TILE // 8), axis=0)
    i = lax.broadcasted_iota(jnp.int32, (Q_TILE, ROLL_W), 0)
    for bit in range(Q_TILE.bit_length() - 1):
        tile = jnp.where(((i >> bit) & 1) == 1, pltpu.roll(tile, 1 << bit, 1), tile)
    tile = tile[:, :KV_WIN]
    i = lax.broadcasted_iota(jnp.int32, (Q_TILE, KV_WIN), 0)
    j = lax.broadcasted_iota(jnp.int32, (Q_TILE, KV_WIN), 1)
    first = (i // CHUNK) * CHUNK
    in_band = (j >= first) & (j < first + KV_PAD + CHUNK)
    out_ref[0] = jnp.where(in_band, tile, -jnp.inf)


def _bias_tiles(rel_bias):
    nh = rel_bias.shape[0]
    return pl.pallas_call(
        _bias_kernel,
        grid=(nh,),
        in_specs=[pl.BlockSpec(memory_space=pltpu.SMEM)],
        out_specs=pl.BlockSpec((1, Q_TILE, KV_WIN), lambda h: (h, 0, 0)),
        out_shape=jax.ShapeDtypeStruct((nh, Q_TILE, KV_WIN), F32),
        compiler_params=_params(1),
        name="rel_bias_tiles",
    )(rel_bias)


def _attn_kernel(q_ref, k_ref, v_ref, bias_ref, o_ref):
    lane_head = lax.broadcasted_iota(jnp.int32, (1, GROUP_W), 1) // HEAD_DIM
    head_lanes = [lane_head == h for h in range(HEADS_PER_GROUP)]
    col = lax.broadcasted_iota(jnp.int32, (1, KV_WIN), 1)
    rows = HEADS_PER_GROUP * Q_TILE
    n_groups = q_ref.shape[-1] // GROUP_W
    for tile, g in [(a, b) for a in range(TILES_PER_STEP) for b in range(n_groups)]:
        start = pl.multiple_of((pl.program_id(1) * TILES_PER_STEP + tile) * Q_TILE, Q_TILE)
        valid = col >= KV_PAD - start
        qrows = slice(tile * Q_TILE, (tile + 1) * Q_TILE)
        cols = slice(g * GROUP_W, (g + 1) * GROUP_W)
        q = q_ref[0, qrows, cols]
        kw = k_ref[0, pl.ds(start, KV_WIN), cols]
        vw = v_ref[0, pl.ds(start, KV_WIN), cols]
        qs = jnp.concatenate([jnp.where(m, q, jnp.zeros_like(q)) for m in head_lanes], axis=0)
        s = lax.dot_general(qs, kw, (((1,), (1,)), ((), ())), preferred_element_type=F32)
        s = s + bias_ref[g * HEADS_PER_GROUP:(g + 1) * HEADS_PER_GROUP].reshape(rows, KV_WIN)
        s = jnp.where(valid, s, -jnp.inf)
        p = jnp.exp(s - jnp.max(s, axis=-1, keepdims=True))
        inv = 1.0 / jnp.sum(p, axis=-1, keepdims=True)
        r = jnp.dot(p.astype(BF16), vw, preferred_element_type=F32) * inv
        o = jnp.zeros((Q_TILE, GROUP_W), F32)
        for h, m in enumerate(head_lanes):
            o = jnp.where(m, r[h * Q_TILE:(h + 1) * Q_TILE], o)
        o_ref[0, qrows, cols] = o.astype(BF16)


def _attention(qkv, bias_tiles, seq):
    b, s_pad, d3 = qkv.shape
    d = d3 // 3
    step_rows = TILES_PER_STEP * Q_TILE
    assert KV_PAD % step_rows == 0 and seq % step_rows == 0
    kv_spec = lambda which: pl.BlockSpec((1, s_pad, d), lambda bi, t: (bi, 0, which),
                                         pipeline_mode=pl.Buffered(1))
    return pl.pallas_call(
        _attn_kernel,
        grid=(b, seq // step_rows),
        in_specs=[pl.BlockSpec((1, step_rows, d), lambda bi, t: (bi, t + KV_PAD // step_rows, 0)),
                  kv_spec(1), kv_spec(2), _resident(bias_tiles.shape)],
        out_specs=pl.BlockSpec((1, step_rows, d), lambda bi, t: (bi, t, 0)),
        out_shape=jax.ShapeDtypeStruct((b, seq, d), BF16),
        compiler_params=_params(2),
        name="band_attention",
    )(qkv, qkv, qkv, bias_tiles)


def kernel(x, norm_mix_g, norm_ffn_g, final_g, a_w_in, a_v_gain, a_w_s, a_b_s, a_w_out,
           b_w_qkv, b_rel_bias, b_w_out, ffn_w_gate, ffn_w_up, ffn_w_down):
    bsz, seq, d = x.shape
    depth = norm_mix_g.shape[0]
    n_mixers = 2
    assert d == N_HEADS * HEAD_DIM and seq % TOKEN_TILE == 0

    bf = lambda w: w.astype(BF16)
    a_w_in, a_w_out, b_w_qkv, b_w_out = bf(a_w_in), bf(a_w_out), bf(b_w_qkv), bf(b_w_out)
    ffn_w_gate, ffn_w_up, ffn_w_down = bf(ffn_w_gate), bf(ffn_w_up), bf(ffn_w_down)

    x2d = x.reshape(bsz * seq, d)
    for i in range(depth):
        j = i // n_mixers
        final = final_g if i == depth - 1 else None
        if i % n_mixers == 0:
            x2d = _gmlp(x2d, norm_mix_g[i], a_w_in[j], a_v_gain[j], a_w_s[j], a_b_s[j], a_w_out[j])
            x2d = _ffn(x2d, norm_ffn_g[i], ffn_w_gate[i], ffn_w_up[i], ffn_w_down[i], final_g=final)
        else:
            qkv = _qkv(x2d.reshape(bsz, seq, d), norm_mix_g[i], b_w_qkv[j])
            o = _attention(qkv, _bias_tiles(b_rel_bias[j]), seq)
            x2d = _ffn(x2d, norm_ffn_g[i], ffn_w_gate[i], ffn_w_up[i], ffn_w_down[i],
                       pre=(o.reshape(bsz * seq, d), b_w_out[j]), final_g=final)
    return x2d.reshape(bsz, seq, d)
```

```python
import functools

import numpy as np
import jax
import jax.numpy as jnp
from jax import lax
from jax.experimental import pallas as pl
from jax.experimental.pallas import tpu as pltpu

F32 = jnp.float32
BF16 = jnp.bfloat16

EPS = 1e-6
CHUNK = 64
SGU_BLOCK = 128
SGU_GROUPS = 8
N_HEADS = 16
HEAD_DIM = 64
LEFT_CHUNKS = 8
REL_MIN = -(CHUNK - 1)
REL_MAX = 128
N_REL = REL_MAX - REL_MIN + 1

Q_TILE = 2 * CHUNK
KV_PAD = LEFT_CHUNKS * CHUNK
KV_WIN = KV_PAD + Q_TILE
HEADS_PER_GROUP = 4
GROUP_W = HEADS_PER_GROUP * HEAD_DIM
TILES_PER_STEP = 2
ROLL_W = 1024

TOKEN_TILE = 512
COL_CHUNK = 512
V7X_VMEM_LIMIT_BYTES = 60 * 1024 * 1024


def _rms(x, g):
    ms = jnp.mean(x * x, axis=-1, keepdims=True)
    return x * lax.rsqrt(ms + EPS) * g


def _gelu_tanh(x):
    c = np.float32(np.sqrt(2.0 / np.pi))
    return x * (0.5 * (1.0 + jnp.tanh(c * (x + np.float32(0.044715) * (x * x * x)))))


def _mm(a, w):
    return jnp.dot(a, w.astype(BF16), preferred_element_type=F32)


def _resident(shape):
    nd = len(shape)
    return pl.BlockSpec(shape, lambda *_: (0,) * nd, pipeline_mode=pl.Buffered(1))


def _layer(stacked, layer):
    rest = stacked.shape[1:]
    return pl.BlockSpec((None,) + rest, lambda *_: (layer,) + (0,) * len(rest),
                        pipeline_mode=pl.Buffered(1))


def _rows(stacked):
    return stacked.reshape(stacked.shape[0], 1, stacked.shape[1])


def _params(n_axes):
    return pltpu.CompilerParams(
        dimension_semantics=("arbitrary",) * n_axes,
        vmem_limit_bytes=V7X_VMEM_LIMIT_BYTES,
    )


def _ffn_kernel(*refs, has_pre, has_final, chunks):
    it = iter(refs)
    x_ref = next(it)
    if has_pre:
        o_ref, wo_ref = next(it), next(it)
    g_ref, wg_ref, wu_ref, wd_ref = next(it), next(it), next(it), next(it)
    if has_final:
        fg_ref = next(it)
    out_ref = next(it)

    x = x_ref[...]
    if has_pre:
        x = x + _mm(o_ref[...], wo_ref[...])
    hn = _rms(x, g_ref[...]).astype(BF16)
    acc = x
    for c0, cw in chunks:
        gate = _mm(hn, wg_ref[:, c0:c0 + cw])
        up = _mm(hn, wu_ref[:, c0:c0 + cw])
        h = (gate * (0.5 * (1.0 + jnp.tanh(0.5 * gate))) * up).astype(BF16)
        acc = acc + _mm(h, wd_ref[c0:c0 + cw, :])
    if has_final:
        acc = _rms(acc, fg_ref[...])
    out_ref[...] = acc


def _ffn(x2d, layer, g_all, wg_all, wu_all, wd_all, pre=None, final_g=None):
    m, d = x2d.shape
    f = wg_all.shape[-1]
    chunks = tuple((c0, min(COL_CHUNK, f - c0)) for c0 in range(0, f, COL_CHUNK))
    row_spec = pl.BlockSpec((TOKEN_TILE, d), lambda i: (i, 0))
    args, specs = [x2d], [row_spec]
    if pre is not None:
        o2d, wo_all, wo_layer = pre
        args += [o2d, wo_all]
        specs += [row_spec, _layer(wo_all, wo_layer)]
    g_rows = _rows(g_all)
    args += [g_rows, wg_all, wu_all, wd_all]
    specs += [_layer(g_rows, layer), _layer(wg_all, layer), _layer(wu_all, layer),
              _layer(wd_all, layer)]
    if final_g is not None:
        args.append(final_g.reshape(1, d))
        specs.append(_resident((1, d)))
    return pl.pallas_call(
        functools.partial(_ffn_kernel, has_pre=pre is not None,
                          has_final=final_g is not None, chunks=chunks),
        grid=(m // TOKEN_TILE,),
        in_specs=specs,
        out_specs=row_spec,
        out_shape=jax.ShapeDtypeStruct((m, d), F32),
        compiler_params=_params(1),
        name="ffn",
    )(*args)


def _gmlp_kernel(x_ref, g_ref, win_ref, vg_ref, ws_ref, bs_ref, wout_ref, out_ref,
                 u_scr, v_scr, y_scr):
    tm = x_ref.shape[0]
    hid = u_scr.shape[1]
    cw = COL_CHUNK
    x = x_ref[...]
    hn = _rms(x, g_ref[...]).astype(BF16)
    ssq = jnp.zeros((tm, 1), F32)
    for c in range(2 * hid // cw):
        t = _gelu_tanh(_mm(hn, win_ref[:, c * cw:(c + 1) * cw]))
        if c * cw < hid:
            u_scr[:, c * cw:(c + 1) * cw] = t
        else:
            v_scr[:, c * cw - hid:(c + 1) * cw - hid] = t
            ssq = ssq + jnp.sum(t * t, axis=-1, keepdims=True)
    inv = lax.rsqrt(ssq * (1.0 / hid) + EPS)

    p_chunk = lax.broadcasted_iota(jnp.int32, (SGU_BLOCK, SGU_BLOCK), 0) // CHUNK
    q_chunk = lax.broadcasted_iota(jnp.int32, (SGU_BLOCK, SGU_BLOCK), 1) // CHUNK
    causal = q_chunk <= p_chunk
    gw = hid // SGU_GROUPS
    for grp in range(SGU_GROUPS):
        cols = slice(grp * gw, (grp + 1) * gw)
        w_g = jnp.where(causal, ws_ref[grp], 0.0).astype(BF16)
        bias = bs_ref[:, grp:grp + 1]
        for n in range(tm // SGU_BLOCK):
            rows = slice(n * SGU_BLOCK, (n + 1) * SGU_BLOCK)
            vn = (v_scr[rows, cols] * inv[rows] * vg_ref[:, cols]).astype(BF16)
            mixed = jnp.dot(w_g, vn, preferred_element_type=F32) + bias
            y_scr[rows, cols] = (u_scr[rows, cols] * mixed).astype(BF16)
    out_ref[...] = x + _mm(y_scr[...], wout_ref[...])


def _gmlp(x2d, layer, j, g_all, w_in_all, v_gain_all, w_s_all, b_s_all, w_out_all):
    m, d = x2d.shape
    hid = w_out_all.shape[1]
    row_spec = pl.BlockSpec((TOKEN_TILE, d), lambda i: (i, 0))
    g_rows, vg_rows = _rows(g_all), _rows(v_gain_all)
    bs_t = jnp.swapaxes(b_s_all, 1, 2)
    return pl.pallas_call(
        _gmlp_kernel,
        grid=(m // TOKEN_TILE,),
        in_specs=[row_spec, _layer(g_rows, layer), _layer(w_in_all, j), _layer(vg_rows, j),
                  _layer(w_s_all, j), _layer(bs_t, j), _layer(w_out_all, j)],
        out_specs=row_spec,
        out_shape=jax.ShapeDtypeStruct((m, d), F32),
        scratch_shapes=[pltpu.VMEM((TOKEN_TILE, hid), F32), pltpu.VMEM((TOKEN_TILE, hid), F32),
                        pltpu.VMEM((TOKEN_TILE, hid), BF16)],
        compiler_params=_params(1),
        name="gmlp",
    )(x2d, g_rows, w_in_all, vg_rows, w_s_all, bs_t, w_out_all)


def _qkv_kernel(x_ref, g_ref, w_ref, out_ref):
    d = x_ref.shape[-1]

    @pl.when(pl.program_id(1) == 0)
    def _():
        out_ref[...] = jnp.zeros(out_ref.shape, out_ref.dtype)

    @pl.when(pl.program_id(1) > 0)
    def _():
        hn = _rms(x_ref[0], g_ref[...]).astype(BF16)
        for c in range(3 * d // COL_CHUNK):
            cols = slice(c * COL_CHUNK, (c + 1) * COL_CHUNK)
            t = _mm(hn, w_ref[:, cols])
            if c * COL_CHUNK < d:
                t = t * (HEAD_DIM ** -0.5)
            out_ref[0, :, cols] = t.astype(BF16)


def _qkv(x3d, layer, j, g_all, w_qkv_all):
    b, s, d = x3d.shape
    assert KV_PAD == TOKEN_TILE
    g_rows = _rows(g_all)
    return pl.pallas_call(
        _qkv_kernel,
        grid=(b, s // TOKEN_TILE + 1),
        in_specs=[pl.BlockSpec((1, TOKEN_TILE, d), lambda bi, r: (bi, jnp.maximum(r - 1, 0), 0)),
                  _layer(g_rows, layer), _layer(w_qkv_all, j)],
        out_specs=pl.BlockSpec((1, TOKEN_TILE, 3 * d), lambda bi, r: (bi, r, 0)),
        out_shape=jax.ShapeDtypeStruct((b, s + KV_PAD, 3 * d), BF16),
        compiler_params=_params(2),
        name="qkv_proj",
    )(x3d, g_rows, w_qkv_all)


def _bias_kernel(rb_ref, out_ref, *, layer):
    h = pl.program_id(0)
    m = lax.broadcasted_iota(jnp.int32, (8, ROLL_W), 1)
    idx = jnp.where(m < ROLL_W - Q_TILE, jnp.clip(KV_PAD - m, REL_MIN, REL_MAX) - REL_MIN, N_REL - 1)
    row = jnp.zeros((8, ROLL_W), F32)
    for k in range(N_REL):
        row = jnp.where(idx == k, rb_ref[layer, h, k], row)
    tile = jnp.concatenate([row] * (Q_TILE // 8), axis=0)
    i = lax.broadcasted_iota(jnp.int32, (Q_TILE, ROLL_W), 0)
    for bit in range(Q_TILE.bit_length() - 1):
        tile = jnp.where(((i >> bit) & 1) == 1, pltpu.roll(tile, 1 << bit, 1), tile)
    tile = tile[:, :KV_WIN]
    i = lax.broadcasted_iota(jnp.int32, (Q_TILE, KV_WIN), 0)
    j = lax.broadcasted_iota(jnp.int32, (Q_TILE, KV_WIN), 1)
    first = (i // CHUNK) * CHUNK
    in_band = (j >= first) & (j < first + KV_PAD + CHUNK)
    out_ref[0] = jnp.where(in_band, tile, -jnp.inf)


def _bias_tiles(rel_bias_all, layer):
    nh = rel_bias_all.shape[1]
    return pl.pallas_call(
        functools.partial(_bias_kernel, layer=layer),
        grid=(nh,),
        in_specs=[pl.BlockSpec(memory_space=pltpu.SMEM)],
        out_specs=pl.BlockSpec((1, Q_TILE, KV_WIN), lambda h: (h, 0, 0)),
        out_shape=jax.ShapeDtypeStruct((nh, Q_TILE, KV_WIN), F32),
        compiler_params=_params(1),
        name="rel_bias_tiles",
    )(rel_bias_all)


def _attn_kernel(q_ref, k_ref, v_ref, bias_ref, o_ref):
    lane_head = lax.broadcasted_iota(jnp.int32, (1, GROUP_W), 1) // HEAD_DIM
    head_lanes = [lane_head == h for h in range(HEADS_PER_GROUP)]
    col = lax.broadcasted_iota(jnp.int32, (1, KV_WIN), 1)
    rows = HEADS_PER_GROUP * Q_TILE
    n_groups = q_ref.shape[-1] // GROUP_W
    for tile, g in [(a, b) for a in range(TILES_PER_STEP) for b in range(n_groups)]:
        start = pl.multiple_of((pl.program_id(1) * TILES_PER_STEP + tile) * Q_TILE, Q_TILE)
        valid = col >= KV_PAD - start
        qrows = slice(tile * Q_TILE, (tile + 1) * Q_TILE)
        cols = slice(g * GROUP_W, (g + 1) * GROUP_W)
        q = q_ref[0, qrows, cols]
        kw = k_ref[0, pl.ds(start, KV_WIN), cols]
        vw = v_ref[0, pl.ds(start, KV_WIN), cols]
        qs = jnp.concatenate([jnp.where(m, q, jnp.zeros_like(q)) for m in head_lanes], axis=0)
        s = lax.dot_general(qs, kw, (((1,), (1,)), ((), ())), preferred_element_type=F32)
        s = s + bias_ref[g * HEADS_PER_GROUP:(g + 1) * HEADS_PER_GROUP].reshape(rows, KV_WIN)
        s = jnp.where(valid, s, -jnp.inf)
        p = jnp.exp(s - jnp.max(s, axis=-1, keepdims=True))
        inv = 1.0 / jnp.sum(p, axis=-1, keepdims=True)
        r = jnp.dot(p.astype(BF16), vw, preferred_element_type=F32) * inv
        o = jnp.zeros((Q_TILE, GROUP_W), F32)
        for h, m in enumerate(head_lanes):
            o = jnp.where(m, r[h * Q_TILE:(h + 1) * Q_TILE], o)
        o_ref[0, qrows, cols] = o.astype(BF16)


def _attention(qkv, bias_tiles, seq):
    b, s_pad, d3 = qkv.shape
    d = d3 // 3
    step_rows = TILES_PER_STEP * Q_TILE
    assert KV_PAD % step_rows == 0 and seq % step_rows == 0
    kv_spec = lambda which: pl.BlockSpec((1, s_pad, d), lambda bi, t: (bi, 0, which),
                                         pipeline_mode=pl.Buffered(1))
    return pl.pallas_call(
        _attn_kernel,
        grid=(b, seq // step_rows),
        in_specs=[pl.BlockSpec((1, step_rows, d), lambda bi, t: (bi, t + KV_PAD // step_rows, 0)),
                  kv_spec(1), kv_spec(2), _resident(bias_tiles.shape)],
        out_specs=pl.BlockSpec((1, step_rows, d), lambda bi, t: (bi, t, 0)),
        out_shape=jax.ShapeDtypeStruct((b, seq, d), BF16),
        compiler_params=_params(2),
        name="band_attention",
    )(qkv, qkv, qkv, bias_tiles)


def kernel(x, norm_mix_g, norm_ffn_g, final_g, a_w_in, a_v_gain, a_w_s, a_b_s, a_w_out,
           b_w_qkv, b_rel_bias, b_w_out, ffn_w_gate, ffn_w_up, ffn_w_down):
    bsz, seq, d = x.shape
    depth = norm_mix_g.shape[0]
    n_mixers = 2
    assert d == N_HEADS * HEAD_DIM and seq % TOKEN_TILE == 0

    x2d = x.reshape(bsz * seq, d)
    for i in range(depth):
        j = i // n_mixers
        final = final_g if i == depth - 1 else None
        ffn = functools.partial(_ffn, layer=i, g_all=norm_ffn_g, wg_all=ffn_w_gate,
                                wu_all=ffn_w_up, wd_all=ffn_w_down, final_g=final)
        if i % n_mixers == 0:
            x2d = _gmlp(x2d, i, j, norm_mix_g, a_w_in, a_v_gain, a_w_s, a_b_s, a_w_out)
            x2d = ffn(x2d)
        else:
            qkv = _qkv(x2d.reshape(bsz, seq, d), i, j, norm_mix_g, b_w_qkv)
            o = _attention(qkv, _bias_tiles(b_rel_bias, j), seq)
            x2d = ffn(x2d, pre=(o.reshape(bsz * seq, d), b_w_out, j))
    return x2d.reshape(bsz, seq, d)
```

```python
import functools

import numpy as np
import jax
import jax.numpy as jnp
from jax import lax
from jax.experimental import pallas as pl
from jax.experimental.pallas import tpu as pltpu

F32 = jnp.float32
BF16 = jnp.bfloat16

EPS = 1e-6
CHUNK = 64
SGU_BLOCK = 128
SGU_GROUPS = 8
N_HEADS = 16
HEAD_DIM = 64
LEFT_CHUNKS = 8
REL_MIN = -(CHUNK - 1)
REL_MAX = 128
N_REL = REL_MAX - REL_MIN + 1

Q_TILE = 2 * CHUNK
KV_PAD = LEFT_CHUNKS * CHUNK
KV_WIN = KV_PAD + Q_TILE
HEADS_PER_GROUP = 4
GROUP_W = HEADS_PER_GROUP * HEAD_DIM
TILES_PER_STEP = 2
LOG2E = np.float32(np.log2(np.e))
Q_SCALE = np.float32(HEAD_DIM ** -0.5) * LOG2E
ROLL_W = 1024

TOKEN_TILE = 512
COL_CHUNK = 512
V7X_VMEM_LIMIT_BYTES = 60 * 1024 * 1024


def _rms(x, g):
    ms = jnp.mean(x * x, axis=-1, keepdims=True)
    return x * lax.rsqrt(ms + EPS) * g


def _gelu_tanh(x):
    c = np.float32(np.sqrt(2.0 / np.pi))
    return x * (0.5 * (1.0 + jnp.tanh(c * (x + np.float32(0.044715) * (x * x * x)))))


def _mm(a, w):
    return jnp.dot(a, w.astype(BF16), preferred_element_type=F32)


def _resident(shape):
    nd = len(shape)
    return pl.BlockSpec(shape, lambda *_: (0,) * nd, pipeline_mode=pl.Buffered(1))


def _layer(stacked, layer):
    rest = stacked.shape[1:]
    return pl.BlockSpec((None,) + rest, lambda *_: (layer,) + (0,) * len(rest),
                        pipeline_mode=pl.Buffered(1))


def _rows(stacked):
    return stacked.reshape(stacked.shape[0], 1, stacked.shape[1])


def _params(n_axes):
    return pltpu.CompilerParams(
        dimension_semantics=("arbitrary",) * n_axes,
        vmem_limit_bytes=V7X_VMEM_LIMIT_BYTES,
    )


def _ffn_kernel(*refs, has_pre, has_final, chunks):
    it = iter(refs)
    x_ref = next(it)
    if has_pre:
        o_ref, wo_ref = next(it), next(it)
    g_ref, wg_ref, wu_ref, wd_ref = next(it), next(it), next(it), next(it)
    if has_final:
        fg_ref = next(it)
    out_ref = next(it)

    x = x_ref[...]
    if has_pre:
        x = x + _mm(o_ref[...], wo_ref[...])
    hn = _rms(x, g_ref[...]).astype(BF16)
    acc = x
    for c0, cw in chunks:
        gate = _mm(hn, wg_ref[:, c0:c0 + cw])
        up = _mm(hn, wu_ref[:, c0:c0 + cw])
        h = (gate * (0.5 * (1.0 + jnp.tanh(0.5 * gate))) * up).astype(BF16)
        acc = acc + _mm(h, wd_ref[c0:c0 + cw, :])
    if has_final:
        acc = _rms(acc, fg_ref[...])
    out_ref[...] = acc


def _ffn(x2d, layer, g_all, wg_all, wu_all, wd_all, pre=None, final_g=None):
    m, d = x2d.shape
    f = wg_all.shape[-1]
    chunks = tuple((c0, min(COL_CHUNK, f - c0)) for c0 in range(0, f, COL_CHUNK))
    row_spec = pl.BlockSpec((TOKEN_TILE, d), lambda i: (i, 0))
    args, specs = [x2d], [row_spec]
    if pre is not None:
        o2d, wo_all, wo_layer = pre
        args += [o2d, wo_all]
        specs += [row_spec, _layer(wo_all, wo_layer)]
    g_rows = _rows(g_all)
    args += [g_rows, wg_all, wu_all, wd_all]
    specs += [_layer(g_rows, layer), _layer(wg_all, layer), _layer(wu_all, layer),
              _layer(wd_all, layer)]
    if final_g is not None:
        args.append(final_g.reshape(1, d))
        specs.append(_resident((1, d)))
    return pl.pallas_call(
        functools.partial(_ffn_kernel, has_pre=pre is not None,
                          has_final=final_g is not None, chunks=chunks),
        grid=(m // TOKEN_TILE,),
        in_specs=specs,
        out_specs=row_spec,
        out_shape=jax.ShapeDtypeStruct((m, d), F32),
        compiler_params=_params(1),
        name="ffn",
    )(*args)


def _gmlp_kernel(x_ref, g_ref, win_ref, vg_ref, ws_ref, bs_ref, wout_ref, out_ref,
                 u_scr, v_scr, y_scr):
    tm = x_ref.shape[0]
    hid = u_scr.shape[1]
    cw = COL_CHUNK
    x = x_ref[...]
    hn = _rms(x, g_ref[...]).astype(BF16)
    ssq = jnp.zeros((tm, 1), F32)
    for c in range(2 * hid // cw):
        t = _gelu_tanh(_mm(hn, win_ref[:, c * cw:(c + 1) * cw]))
        if c * cw < hid:
            u_scr[:, c * cw:(c + 1) * cw] = t
        else:
            v_scr[:, c * cw - hid:(c + 1) * cw - hid] = t
            ssq = ssq + jnp.sum(t * t, axis=-1, keepdims=True)
    inv = lax.rsqrt(ssq * (1.0 / hid) + EPS)

    p_chunk = lax.broadcasted_iota(jnp.int32, (SGU_BLOCK, SGU_BLOCK), 0) // CHUNK
    q_chunk = lax.broadcasted_iota(jnp.int32, (SGU_BLOCK, SGU_BLOCK), 1) // CHUNK
    causal = q_chunk <= p_chunk
    gw = hid // SGU_GROUPS
    for grp in range(SGU_GROUPS):
        cols = slice(grp * gw, (grp + 1) * gw)
        w_g = jnp.where(causal, ws_ref[grp], 0.0).astype(BF16)
        bias = bs_ref[:, grp:grp + 1]
        for n in range(tm // SGU_BLOCK):
            rows = slice(n * SGU_BLOCK, (n + 1) * SGU_BLOCK)
            vn = (v_scr[rows, cols] * inv[rows] * vg_ref[:, cols]).astype(BF16)
            mixed = jnp.dot(w_g, vn, preferred_element_type=F32) + bias
            y_scr[rows, cols] = (u_scr[rows, cols] * mixed).astype(BF16)
    out_ref[...] = x + _mm(y_scr[...], wout_ref[...])


def _gmlp(x2d, layer, j, g_all, w_in_all, v_gain_all, w_s_all, b_s_all, w_out_all):
    m, d = x2d.shape
    hid = w_out_all.shape[1]
    row_spec = pl.BlockSpec((TOKEN_TILE, d), lambda i: (i, 0))
    g_rows, vg_rows = _rows(g_all), _rows(v_gain_all)
    bs_t = jnp.swapaxes(b_s_all, 1, 2)
    return pl.pallas_call(
        _gmlp_kernel,
        grid=(m // TOKEN_TILE,),
        in_specs=[row_spec, _layer(g_rows, layer), _layer(w_in_all, j), _layer(vg_rows, j),
                  _layer(w_s_all, j), _layer(bs_t, j), _layer(w_out_all, j)],
        out_specs=row_spec,
        out_shape=jax.ShapeDtypeStruct((m, d), F32),
        scratch_shapes=[pltpu.VMEM((TOKEN_TILE, hid), F32), pltpu.VMEM((TOKEN_TILE, hid), F32),
                        pltpu.VMEM((TOKEN_TILE, hid), BF16)],
        compiler_params=_params(1),
        name="gmlp",
    )(x2d, g_rows, w_in_all, vg_rows, w_s_all, bs_t, w_out_all)


def _qkv_kernel(x_ref, g_ref, w_ref, out_ref):
    d = x_ref.shape[-1]

    @pl.when(pl.program_id(1) == 0)
    def _():
        out_ref[...] = jnp.zeros(out_ref.shape, out_ref.dtype)

    @pl.when(pl.program_id(1) > 0)
    def _():
        hn = _rms(x_ref[0], g_ref[...]).astype(BF16)
        for c in range(3 * d // COL_CHUNK):
            cols = slice(c * COL_CHUNK, (c + 1) * COL_CHUNK)
            t = _mm(hn, w_ref[:, cols])
            if c * COL_CHUNK < d:
                t = t * Q_SCALE
            out_ref[0, :, cols] = t.astype(BF16)


def _qkv(x3d, layer, j, g_all, w_qkv_all):
    b, s, d = x3d.shape
    assert KV_PAD == TOKEN_TILE
    g_rows = _rows(g_all)
    return pl.pallas_call(
        _qkv_kernel,
        grid=(b, s // TOKEN_TILE + 1),
        in_specs=[pl.BlockSpec((1, TOKEN_TILE, d), lambda bi, r: (bi, jnp.maximum(r - 1, 0), 0)),
                  _layer(g_rows, layer), _layer(w_qkv_all, j)],
        out_specs=pl.BlockSpec((1, TOKEN_TILE, 3 * d), lambda bi, r: (bi, r, 0)),
        out_shape=jax.ShapeDtypeStruct((b, s + KV_PAD, 3 * d), BF16),
        compiler_params=_params(2),
        name="qkv_proj",
    )(x3d, g_rows, w_qkv_all)


def _bias_kernel(rb_ref, out_ref, *, layer):
    h = pl.program_id(0)
    lo = KV_PAD - REL_MAX
    mid_w = 2 * 128
    assert lo % 128 == 0 and N_REL <= mid_w
    idx = jnp.clip(REL_MAX - REL_MIN - lax.broadcasted_iota(jnp.int32, (8, mid_w), 1), 0, N_REL - 1)
    mid = jnp.zeros((8, mid_w), F32)
    for k in range(N_REL):
        mid = jnp.where(idx == k, rb_ref[layer, h, k], mid)
    left = jnp.full((8, lo), rb_ref[layer, h, N_REL - 1], F32)
    right = jnp.full((8, ROLL_W - Q_TILE - lo - mid_w), rb_ref[layer, h, 0], F32)
    wrap = jnp.full((8, Q_TILE), rb_ref[layer, h, N_REL - 1], F32)
    row = jnp.concatenate([left, mid, right, wrap], axis=1)
    tile = jnp.concatenate([row] * (Q_TILE // 8), axis=0)
    tile = pltpu.roll(tile, 0, 1, stride=1, stride_axis=0)
    tile = tile[:, :KV_WIN]
    i = lax.broadcasted_iota(jnp.int32, (Q_TILE, KV_WIN), 0)
    j = lax.broadcasted_iota(jnp.int32, (Q_TILE, KV_WIN), 1)
    first = (i // CHUNK) * CHUNK
    in_band = (j >= first) & (j < first + KV_PAD + CHUNK)
    out_ref[0] = jnp.where(in_band, tile * LOG2E, -jnp.inf)


def _bias_tiles(rel_bias_all, layer):
    nh = rel_bias_all.shape[1]
    return pl.pallas_call(
        functools.partial(_bias_kernel, layer=layer),
        grid=(nh,),
        in_specs=[pl.BlockSpec(memory_space=pltpu.SMEM)],
        out_specs=pl.BlockSpec((1, Q_TILE, KV_WIN), lambda h: (h, 0, 0)),
        out_shape=jax.ShapeDtypeStruct((nh, Q_TILE, KV_WIN), F32),
        compiler_params=_params(1),
        name="rel_bias_tiles",
    )(rel_bias_all)


def _attn_kernel(*refs, first_step, last_step, masked):
    if masked:
        q_ref, k_ref, v_ref, bias_ref, o_ref = refs
    else:
        q_ref, qn_ref, k_ref, v_ref, bias_ref, _, o_ref, p_scr, inv_scr = refs
    lane_head = lax.broadcasted_iota(jnp.int32, (1, GROUP_W), 1) // HEAD_DIM
    head_lanes = [lane_head == h for h in range(HEADS_PER_GROUP)]
    col = lax.broadcasted_iota(jnp.int32, (1, KV_WIN), 1)
    rows = HEADS_PER_GROUP * Q_TILE
    n_groups = q_ref.shape[-1] // GROUP_W
    step = pl.program_id(1) + first_step
    units = [(a, b) for a in range(TILES_PER_STEP) for b in range(n_groups)]

    def window_start(at_step, tile):
        return pl.multiple_of((at_step * TILES_PER_STEP + tile) * Q_TILE, Q_TILE)

    def probabilities(src_ref, at_step, tile, g):
        start = window_start(at_step, tile)
        cols = slice(g * GROUP_W, (g + 1) * GROUP_W)
        q = src_ref[0, tile * Q_TILE:(tile + 1) * Q_TILE, cols]
        kw = k_ref[0, pl.ds(start, KV_WIN), cols]
        qs = jnp.concatenate([jnp.where(m, q, jnp.zeros_like(q)) for m in head_lanes], axis=0)
        s = lax.dot_general(qs, kw, (((1,), (1,)), ((), ())), preferred_element_type=F32)
        s = s + bias_ref[g * HEADS_PER_GROUP:(g + 1) * HEADS_PER_GROUP].reshape(rows, KV_WIN)
        if masked:
            s = jnp.where(col >= KV_PAD - start, s, -jnp.inf)
        p = jnp.exp2(s - jnp.max(s, axis=-1, keepdims=True))
        return p.astype(BF16), 1.0 / jnp.sum(p, axis=-1, keepdims=True)

    def weighted_values(p, inv, tile, g):
        cols = slice(g * GROUP_W, (g + 1) * GROUP_W)
        vw = v_ref[0, pl.ds(window_start(step, tile), KV_WIN), cols]
        r = jnp.dot(p, vw, preferred_element_type=F32) * inv
        o = jnp.zeros((Q_TILE, GROUP_W), F32)
        for h, m in enumerate(head_lanes):
            o = jnp.where(m, r[h * Q_TILE:(h + 1) * Q_TILE], o)
        o_ref[0, tile * Q_TILE:(tile + 1) * Q_TILE, cols] = o.astype(BF16)

    if masked:
        for tile, g in units:
            weighted_values(*probabilities(q_ref, step, tile, g), tile, g)
        return

    @pl.when(pl.program_id(1) == 0)
    def _():
        p_scr[...], inv_scr[...] = probabilities(q_ref, step, *units[0])

    weighted_values(p_scr[...], inv_scr[...], *units[0])
    for tile, g in units[1:]:
        weighted_values(*probabilities(q_ref, step, tile, g), tile, g)
    p_scr[...], inv_scr[...] = probabilities(qn_ref, jnp.minimum(step + 1, last_step), *units[0])


def _attention(qkv, bias_tiles, seq):
    b, s_pad, d3 = qkv.shape
    d = d3 // 3
    step_rows = TILES_PER_STEP * Q_TILE
    assert KV_PAD % step_rows == 0 and seq % step_rows == 0
    pad_steps = KV_PAD // step_rows
    last_step = seq // step_rows - 1
    out_shape = jax.ShapeDtypeStruct((b, seq, d), BF16)

    def q_spec(first_step, ahead):
        return pl.BlockSpec((1, step_rows, d), lambda bi, t: (
            bi, jnp.minimum(t + first_step + ahead, last_step) + pad_steps, 0))

    def kv_spec(kv_rows, which):
        return pl.BlockSpec((1, kv_rows, d), lambda bi, t: (bi, 0, which),
                            pipeline_mode=pl.Buffered(1))

    head_rows = KV_PAD + pad_steps * step_rows
    head = pl.pallas_call(
        functools.partial(_attn_kernel, first_step=0, last_step=last_step, masked=True),
        grid=(b, pad_steps),
        in_specs=[q_spec(0, 0), kv_spec(head_rows, 1), kv_spec(head_rows, 2),
                  _resident(bias_tiles.shape)],
        out_specs=pl.BlockSpec((1, step_rows, d), lambda bi, t: (bi, t, 0)),
        out_shape=out_shape,
        compiler_params=_params(2),
        name="band_attention_head",
    )(qkv, qkv, qkv, bias_tiles)
    return pl.pallas_call(
        functools.partial(_attn_kernel, first_step=pad_steps, last_step=last_step, masked=False),
        grid=(b, last_step + 1 - pad_steps),
        in_specs=[q_spec(pad_steps, 0), q_spec(pad_steps, 1), kv_spec(s_pad, 1), kv_spec(s_pad, 2),
                  _resident(bias_tiles.shape), pl.BlockSpec(memory_space=pl.ANY)],
        out_specs=pl.BlockSpec((1, step_rows, d), lambda bi, t: (bi, t + pad_steps, 0)),
        out_shape=out_shape,
        input_output_aliases={5: 0},
        scratch_shapes=[pltpu.VMEM((HEADS_PER_GROUP * Q_TILE, KV_WIN), BF16),
                        pltpu.VMEM((HEADS_PER_GROUP * Q_TILE, 1), F32)],
        compiler_params=_params(2),
        name="band_attention",
    )(qkv, qkv, qkv, qkv, bias_tiles, head)


def kernel(x, norm_mix_g, norm_ffn_g, final_g, a_w_in, a_v_gain, a_w_s, a_b_s, a_w_out,
           b_w_qkv, b_rel_bias, b_w_out, ffn_w_gate, ffn_w_up, ffn_w_down):
    bsz, seq, d = x.shape
    depth = norm_mix_g.shape[0]
    n_mixers = 2
    assert d == N_HEADS * HEAD_DIM and seq % TOKEN_TILE == 0

    x2d = x.reshape(bsz * seq, d)
    for i in range(depth):
        j = i // n_mixers
        final = final_g if i == depth - 1 else None
        ffn = functools.partial(_ffn, layer=i, g_all=norm_ffn_g, wg_all=ffn_w_gate,
                                wu_all=ffn_w_up, wd_all=ffn_w_down, final_g=final)
        if i % n_mixers == 0:
            x2d = _gmlp(x2d, i, j, norm_mix_g, a_w_in, a_v_gain, a_w_s, a_b_s, a_w_out)
            x2d = ffn(x2d)
        else:
            qkv = _qkv(x2d.reshape(bsz, seq, d), i, j, norm_mix_g, b_w_qkv)
            o = _attention(qkv, _bias_tiles(b_rel_bias, j), seq)
            x2d = ffn(x2d, pre=(o.reshape(bsz * seq, d), b_w_out, j))
    return x2d.reshape(bsz, seq, d)
```
